```python
import jax, jax.numpy as jnp
from jax import lax
import numpy as np


D_MODEL = 1024
BATCH = 8
SEQ = 2048
DEPTH = 4
DEC_BATCH = 32
DEC_SEQ = 1
PAST_LEN = 8192
PAGE_SIZE = 128

MIX_WIDTH = D_MODEL
HG_WIDTH = MIX_WIDTH // 2
HG_HEAD_DIM = 128
HG_HEADS = HG_WIDTH // HG_HEAD_DIM
ATT_WIDTH = MIX_WIDTH - HG_WIDTH
ATT_HEAD_DIM = 64
ATT_HEADS = ATT_WIDTH // ATT_HEAD_DIM
IDX_HEADS = 8
IDX_DIM = 64
TOPK_MAX = 256
D_FF = 2816
CONV_W = 3
HG_CHUNK = 64
DSA_Q_BLOCK = 64
LN_EPS = 1e-5
RMS_EPS = 1e-6
ALPHA = (2 * DEPTH) ** 0.25
BETA = (8 * DEPTH) ** -0.25
IN_SIZES = (HG_WIDTH, HG_WIDTH, HG_WIDTH, HG_WIDTH,
            ATT_WIDTH, ATT_WIDTH, ATT_WIDTH,
            IDX_HEADS * IDX_DIM, IDX_DIM, IDX_HEADS)

kernel_name = "hymba_hgrn2_dsa_convffn_deepnorm_step"


def split_in(h):
    bounds = [int(b) for b in np.cumsum(IN_SIZES)[:-1]]
    return jnp.split(h, bounds, axis=-1)


def layer_norm(x, g, b):
    xf = x.astype(jnp.float32)
    mu = jnp.mean(xf, axis=-1, keepdims=True)
    var = jnp.mean(jnp.square(xf - mu), axis=-1, keepdims=True)
    return ((xf - mu) * lax.rsqrt(var + LN_EPS) * g + b).astype(x.dtype)


def lower_bounds(lb_param):
    p = jax.nn.softmax(lb_param.astype(jnp.float32), axis=0)
    return jnp.cumsum(p, axis=0) - p[0:1]


def hgrn2_chunked(q, k, v, log_f, s0, chunk):
    B, T, H, _ = q.shape
    n = T // chunk

    def to_chunks(a):
        return a.reshape(B, n, chunk, H, a.shape[-1]).transpose(1, 0, 3, 2, 4)

    causal = jnp.tril(jnp.ones((chunk, chunk), dtype=bool))

    def step(S, inp):
        qb, kb, vb, gb = inp
        cum = jnp.cumsum(gb, axis=2)
        o_inter = jnp.einsum('bhck,bhkv->bhcv', qb * jnp.exp(cum), S)
        diff = cum[:, :, :, None, :] - cum[:, :, None, :, :]
        decay = jnp.exp(jnp.where(causal[:, :, None], diff, -jnp.inf))
        scores = jnp.einsum('bhtk,bhtsk,bhsk->bhts', qb, decay, kb)
        o_intra = jnp.einsum('bhts,bhsv->bhtv', scores, vb)
        last = cum[:, :, -1:, :]
        S_new = jnp.exp(last[:, :, 0, :])[..., None] * S + jnp.einsum(
            'bhsk,bhsv->bhkv', kb * jnp.exp(last - cum), vb)
        return S_new, o_inter + o_intra

    S, o = lax.scan(step, s0, tuple(to_chunks(a) for a in (q, k, v, log_f)))
    o = o.transpose(1, 0, 3, 2, 4).reshape(B, T, H, v.shape[-1])
    return o, S


def hgrn2_recurrence(q, k, v, log_f, s0):
    T = q.shape[1]
    chunk = min(HG_CHUNK, T)
    pad = (-T) % chunk
    if pad:
        padw = ((0, 0), (0, pad), (0, 0), (0, 0))
        q, k, v, log_f = [jnp.pad(a, padw) for a in (q, k, v, log_f)]
    o, S = hgrn2_chunked(q, k, v, log_f, s0.astype(jnp.float32), chunk)
    return o[:, :T], S


def hgrn2_mixer(hq, hf, hi, hg, lb, norm_g, s0):
    B, T, _ = hq.shape
    hf32 = hf.astype(jnp.float32)
    f = lb + (1.0 - lb) * jax.nn.sigmoid(hf32)
    log_f = jnp.log(f)
    k = (1.0 - lb) * jax.nn.sigmoid(-hf32)
    heads = lambda a: a.reshape(B, T, HG_HEADS, HG_HEAD_DIM)
    o, s_new = hgrn2_recurrence(heads(hq), heads(k), heads(hi), heads(log_f), s0)
    o = o * lax.rsqrt(jnp.mean(jnp.square(o), axis=-1, keepdims=True) + RMS_EPS) * norm_g
    o = o.reshape(B, T, HG_WIDTH) * jax.nn.silu(hg.astype(jnp.float32))
    return o.astype(hq.dtype), s_new.astype(s0.dtype)


def indexer_scores(qi, ki, wi):
    dots = jnp.einsum('bqhd,bsd->bqhs', qi.astype(jnp.float32), ki.astype(jnp.float32)) * IDX_DIM ** -0.5
    return jnp.einsum('bqh,bqhs->bqs', wi.astype(jnp.float32) * IDX_HEADS ** -0.5, jax.nn.relu(dots))


def sparse_attend(q, kg, vg, valid):
    s = jnp.einsum('bqhd,bqkhd->bqhk', q, kg).astype(jnp.float32) * ATT_HEAD_DIM ** -0.5
    s = jnp.where(valid[:, :, None, :], s, -jnp.inf)
    p = jax.nn.softmax(s, axis=-1)
    return jnp.einsum('bqhk,bqkhd->bqhd', p.astype(vg.dtype), vg)


def dsa_prompt(q, k, v, qi, ki, wi):
    B, T = q.shape[:2]
    topk = min(TOPK_MAX, T // 4)
    key_pos = jnp.arange(T)
    bidx = jnp.arange(B)[:, None, None]

    def block(start):
        sl = lambda a: lax.dynamic_slice_in_dim(a, start, DSA_Q_BLOCK, axis=1)
        q_pos = start + jnp.arange(DSA_Q_BLOCK)
        score = indexer_scores(sl(qi), ki, sl(wi))
        score = jnp.where(key_pos[None, None, :] <= q_pos[None, :, None], score, -jnp.inf)
        _, idx = lax.top_k(score, topk)
        valid = idx <= q_pos[None, :, None]
        return sparse_attend(sl(q), k[bidx, idx], v[bidx, idx], valid)

    out = lax.map(block, jnp.arange(T // DSA_Q_BLOCK) * DSA_Q_BLOCK)
    return out.transpose(1, 0, 2, 3, 4).reshape(B, T, ATT_HEADS, ATT_HEAD_DIM)


def dsa_sample(l, q, k, v, qi, ki, wi, cache_k, cache_v, cache_kidx, page_table):
    B, Tn = q.shape[:2]
    past = page_table.shape[1] * PAGE_SIZE
    L = past + Tn
    topk = min(TOPK_MAX, L // 4)
    ki_all = jnp.concatenate([cache_kidx[l, page_table].reshape(B, past, IDX_DIM), ki], axis=1)
    q_pos = past + jnp.arange(Tn)
    score = indexer_scores(qi, ki_all, wi)
    score = jnp.where(jnp.arange(L)[None, None, :] <= q_pos[None, :, None], score, -jnp.inf)
    _, idx = lax.top_k(score, topk)
    valid = idx <= q_pos[None, :, None]
    bidx = jnp.arange(B)[:, None, None]
    in_past = (idx < past)[..., None, None]
    p_idx = jnp.minimum(idx, past - 1)
    phys = page_table[bidx, p_idx // PAGE_SIZE]
    off = p_idx % PAGE_SIZE
    n_idx = jnp.clip(idx - past, 0, Tn - 1)
    kg = jnp.where(in_past, cache_k[l, phys, off], k[bidx, n_idx])
    vg = jnp.where(in_past, cache_v[l, phys, off], v[bidx, n_idx])
    return sparse_attend(q, kg, vg, valid)


def conv_ffn(x, w_up, conv_w, conv_b, w_down, buf):
    u = x @ w_up
    T = u.shape[1]
    ext = jnp.concatenate([buf, u], axis=1)
    c = conv_b + sum(ext[:, j:j + T] * conv_w[j] for j in range(CONV_W))
    a, g = jnp.split(c, 2, axis=-1)
    return (a * jax.nn.silu(g)) @ w_down, ext[:, -(CONV_W - 1):]


def trunk(x, hg_states, conv_bufs, attend, w_in, lbs, hg_norm_g, w_out, ln1_g, ln1_b,
          w_up, conv_w, conv_b, w_down, ln2_g, ln2_b):
    B, T, _ = x.shape
    rows_k, rows_v, rows_ki, states, bufs = [], [], [], [], []
    for l in range(DEPTH):
        hq, hf, hi, hg, aq, ak, av, iq, ik, iw = split_in(x @ w_in[l])
        o_hg, s_new = hgrn2_mixer(hq, hf, hi, hg, lbs[l], hg_norm_g[l], hg_states[l])
        q, k, v = [a.reshape(B, T, ATT_HEADS, ATT_HEAD_DIM) for a in (aq, ak, av)]
        o_att = attend(l, q, k, v, iq.reshape(B, T, IDX_HEADS, IDX_DIM), ik, iw)
        mix = jnp.concatenate([o_hg, o_att.reshape(B, T, ATT_WIDTH)], axis=-1) @ w_out[l]
        x = layer_norm(ALPHA * x + mix, ln1_g[l], ln1_b[l])
        ffn, buf_new = conv_ffn(x, w_up[l], conv_w[l], conv_b[l], w_down[l], conv_bufs[l])
        x = layer_norm(ALPHA * x + ffn, ln2_g[l], ln2_b[l])
        rows_k.append(k)
        rows_v.append(v)
        rows_ki.append(ik)
        states.append(s_new)
        bufs.append(buf_new)
    return x, jnp.stack(rows_k), jnp.stack(rows_v), jnp.stack(rows_ki), jnp.stack(states), jnp.stack(bufs)


def setup_inputs(seed: int = 0) -> dict:
    key = jax.random.key(seed)
    ks = jax.random.split(key, 24)
    n_pages = PAST_LEN // PAGE_SIZE
    n_pool = (DEC_BATCH * n_pages * 5) // 4
    nrm = lambda k, shape, s: jax.random.normal(k, shape, jnp.float32) * s
    x_prompt = nrm(ks[0], (BATCH, SEQ, D_MODEL), 1.0)
    x_sample = nrm(ks[1], (DEC_BATCH, DEC_SEQ, D_MODEL), 1.0)
    cache_k = nrm(ks[2], (DEPTH, n_pool, PAGE_SIZE, ATT_HEADS, ATT_HEAD_DIM), 1.0)
    cache_v = nrm(ks[3], (DEPTH, n_pool, PAGE_SIZE, ATT_HEADS, ATT_HEAD_DIM), BETA)
    cache_kidx = nrm(ks[4], (DEPTH, n_pool, PAGE_SIZE, IDX_DIM), 1.0)
    state_hgrn = nrm(ks[5], (DEPTH, DEC_BATCH, HG_HEADS, HG_HEAD_DIM, HG_HEAD_DIM), 0.3)
    state_conv = nrm(ks[6], (DEPTH, DEC_BATCH, CONV_W - 1, 2 * D_FF), BETA)
    page_table = jax.random.permutation(ks[7], n_pool)[: DEC_BATCH * n_pages].reshape(
        DEC_BATCH, n_pages).astype(jnp.int32)
    seg_scales = (1.0, 1.0, BETA, 1.0, 1.0, 1.0, BETA, 1.0, 1.0, 1.0)
    wk = jax.random.split(ks[8], len(IN_SIZES))
    w_in = jnp.concatenate([nrm(wk[i], (DEPTH, D_MODEL, n), s * D_MODEL ** -0.5)
                            for i, (n, s) in enumerate(zip(IN_SIZES, seg_scales))], axis=-1)
    lb_param = nrm(ks[9], (DEPTH, HG_WIDTH), 0.5)
    hg_norm_g = 1.0 + nrm(ks[10], (DEPTH, HG_HEAD_DIM), 0.02)
    w_out = nrm(ks[11], (DEPTH, MIX_WIDTH, D_MODEL), BETA * MIX_WIDTH ** -0.5)
    ln1_g = 1.0 + nrm(ks[12], (DEPTH, D_MODEL), 0.02)
    ln1_b = nrm(ks[13], (DEPTH, D_MODEL), 0.02)
    w_up = nrm(ks[14], (DEPTH, D_MODEL, 2 * D_FF), BETA * D_MODEL ** -0.5)
    conv_w = nrm(ks[15], (DEPTH, CONV_W, 2 * D_FF), CONV_W ** -0.5)
    conv_b = nrm(ks[16], (DEPTH, 2 * D_FF), 0.02)
    w_down = nrm(ks[17], (DEPTH, D_FF, D_MODEL), BETA * D_FF ** -0.5)
    ln2_g = 1.0 + nrm(ks[18], (DEPTH, D_MODEL), 0.02)
    ln2_b = nrm(ks[19], (DEPTH, D_MODEL), 0.02)
    return {'x_prompt': x_prompt, 'x_sample': x_sample, 'cache_k': cache_k, 'cache_v': cache_v,
            'cache_kidx': cache_kidx, 'state_hgrn': state_hgrn, 'state_conv': state_conv,
            'page_table': page_table, 'w_in': w_in, 'lb_param': lb_param, 'hg_norm_g': hg_norm_g,
            'w_out': w_out, 'ln1_g': ln1_g, 'ln1_b': ln1_b, 'w_up': w_up, 'conv_w': conv_w,
            'conv_b': conv_b, 'w_down': w_down, 'ln2_g': ln2_g, 'ln2_b': ln2_b}


def reference(x_prompt, x_sample, cache_k, cache_v, cache_kidx, state_hgrn, state_conv, page_table,
              w_in, lb_param, hg_norm_g, w_out, ln1_g, ln1_b, w_up, conv_w, conv_b, w_down, ln2_g, ln2_b):
    lbs = lower_bounds(lb_param)
    weights = (w_in, lbs, hg_norm_g, w_out, ln1_g, ln1_b, w_up, conv_w, conv_b, w_down, ln2_g, ln2_b)

    B = x_prompt.shape[0]
    hg0 = jnp.zeros((DEPTH, B, HG_HEADS, HG_HEAD_DIM, HG_HEAD_DIM), x_prompt.dtype)
    cb0 = jnp.zeros((DEPTH, B, CONV_W - 1, 2 * D_FF), x_prompt.dtype)
    attend_p = lambda l, q, k, v, qi, ki, wi: dsa_prompt(q, k, v, qi, ki, wi)
    y_prompt, k_p, v_p, ki_p, hg_p, cb_p = trunk(x_prompt, hg0, cb0, attend_p, *weights)

    attend_s = lambda l, q, k, v, qi, ki, wi: dsa_sample(l, q, k, v, qi, ki, wi,
                                                         cache_k, cache_v, cache_kidx, page_table)
    y_sample, k_s, v_s, ki_s, hg_s, cb_s = trunk(x_sample, state_hgrn, state_conv, attend_s, *weights)

    return (y_prompt, y_sample, k_p, v_p, ki_p, hg_p, cb_p, k_s, v_s, ki_s, hg_s, cb_s)
```

```python
import functools

import numpy as np
import jax
import jax.numpy as jnp
from jax import lax
from jax.experimental import pallas as pl
from jax.experimental.pallas import tpu as pltpu

F32 = jnp.float32
BF16 = jnp.bfloat16

HG_HEADS = 4
HG_HEAD_DIM = 128
HG_WIDTH = HG_HEADS * HG_HEAD_DIM
ATT_HEADS = 8
ATT_HEAD_DIM = 64
ATT_WIDTH = ATT_HEADS * ATT_HEAD_DIM
IDX_HEADS = 8
IDX_DIM = 64
TOPK_MAX = 256
CONV_W = 3
PAGE_SIZE = 128
LN_EPS = 1e-5
RMS_EPS = 1e-6

LANES = 128
SUBLANES = 8
VMEM_LIMIT = 56 * 1024 * 1024

COL_HQ, COL_HF, COL_HI, COL_HG = 0, 4, 8, 12
COL_AQ, COL_AK, COL_AV, COL_IQ, COL_IK = 16, 20, 24, 28, 32
N_FULL = 33 * LANES
PCOL_IQ, PCOL_IK = 20, 24
N_TOK = 25 * LANES
N_FEAT = 2 * ATT_WIDTH + IDX_DIM
INT_MIN = -2 ** 31

NT_DIMS = (((1,), (1,)), ((), ()))
TN_DIMS = (((0,), (0,)), ((), ()))


def _params(sem, vmem=VMEM_LIMIT):
    return pltpu.CompilerParams(dimension_semantics=sem, vmem_limit_bytes=vmem)


def _layer_norm(y, g, b):
    mu = jnp.mean(y, axis=-1, keepdims=True)
    d = y - mu
    var = jnp.mean(d * d, axis=-1, keepdims=True)
    return d * lax.rsqrt(var + LN_EPS) * g + b


def _sort_key(score):
    bits = pltpu.bitcast(score, jnp.int32)
    return jnp.where(bits < 0, bits ^ jnp.int32(0x7FFFFFFF), bits)


def _layer_spec(arr, layer):
    zeros = (0,) * (arr.ndim - 1)
    return pl.BlockSpec((None,) + arr.shape[1:], lambda *_: (layer,) + zeros)


def _eye(n):
    return lax.broadcasted_iota(jnp.int32, (n, n), 0) == lax.broadcasted_iota(jnp.int32, (n, n), 1)


def _row_to_col(row):
    n = row.shape[1]
    return jnp.sum(jnp.where(_eye(n), jnp.broadcast_to(row, (n, n)), 0.0), axis=1, keepdims=True)


def _col_to_row(col):
    n = col.shape[0]
    return jnp.sum(jnp.where(_eye(n), jnp.broadcast_to(col, (n, n)), 0.0), axis=0, keepdims=True)


def _inproj_prompt_kernel(x_ref, w_ref, wt_ref, h_ref, kt_ref, vt_ref, kit_ref):
    xb = x_ref[0].astype(BF16)
    n_a = COL_AK * LANES
    h_ref[0, :, :n_a] = jnp.dot(xb, w_ref[:, :n_a], preferred_element_type=F32)
    h_ref[0, :, n_a:] = jnp.dot(xb, w_ref[:, COL_IQ * LANES:], preferred_element_type=F32)
    ft = lax.dot_general(wt_ref[...], xb, NT_DIMS, preferred_element_type=F32)
    kt_ref[0] = ft[:ATT_WIDTH]
    vt_ref[0] = ft[ATT_WIDTH:2 * ATT_WIDTH]
    kit_ref[0] = ft[2 * ATT_WIDTH:]


def _inproj_prompt(x3, w_full, w_feat, layer, tm=256):
    b, t, d = x3.shape
    tm = min(tm, t)
    feat = lambda n: pl.BlockSpec((1, n, tm), lambda bi, ti: (bi, 0, ti))
    return pl.pallas_call(
        _inproj_prompt_kernel,
        grid=(b, t // tm),
        in_specs=[pl.BlockSpec((1, tm, d), lambda bi, ti: (bi, ti, 0)),
                  _layer_spec(w_full, layer), _layer_spec(w_feat, layer)],
        out_specs=[pl.BlockSpec((1, tm, N_TOK), lambda bi, ti: (bi, ti, 0)),
                   feat(ATT_WIDTH), feat(ATT_WIDTH), feat(IDX_DIM)],
        out_shape=[jax.ShapeDtypeStruct((b, t, N_TOK), F32),
                   jax.ShapeDtypeStruct((b, ATT_WIDTH, t), F32),
                   jax.ShapeDtypeStruct((b, ATT_WIDTH, t), F32),
                   jax.ShapeDtypeStruct((b, IDX_DIM, t), F32)],
        compiler_params=_params(("parallel", "parallel")),
        name="inproj_prompt",
    )(x3, w_full, w_feat)


def _inproj_kernel(x_ref, w_ref, h_ref):
    h_ref[...] = jnp.dot(x_ref[...].astype(BF16), w_ref[...], preferred_element_type=F32)


def _inproj_sample(x2d, w_full, layer):
    m = x2d.shape[0]
    return pl.pallas_call(
        _inproj_kernel,
        grid=(1,),
        in_specs=[pl.BlockSpec(x2d.shape, lambda i: (0, 0)), _layer_spec(w_full, layer)],
        out_specs=pl.BlockSpec((m, N_FULL), lambda i: (0, 0)),
        out_shape=jax.ShapeDtypeStruct((m, N_FULL), F32),
        compiler_params=_params(("arbitrary",)),
        name="inproj_sample",
    )(x2d, w_full)


def _hgrn_consts(c):
    nl = int(np.log2(c))
    w = np.zeros((nl + 1, c, c), np.float32)
    m = np.zeros((nl + 1, c, c), np.float32)
    w[0] = np.tril(np.ones((c, c), np.float32))
    m[0] = np.eye(c, dtype=np.float32)
    r = np.arange(c)
    for li in range(1, nl + 1):
        b = 2 ** li
        hb = b // 2
        mid = (r // b) * b + hb
        for t in range(c):
            if t % b >= hb:
                w[li, t, mid[t]:t + 1] = 1.0
            else:
                w[li, t, t + 1:mid[t]] = 1.0
        same = (r[:, None] // b) == (r[None, :] // b)
        m[li] = (same & ((r[:, None] % b) >= hb) & ((r[None, :] % b) < hb)).astype(np.float32)
    return w.reshape((nl + 1) * c, c), m


def _hgrn_kernel(q_ref, f_ref, i_ref, g_ref, lb_ref, ng_ref, w_ref, m_ref, o_ref, s_ref, st_ref,
                 *, chunk, n_chunks, n_levels):
    t = pl.program_id(2)

    @pl.when(t == 0)
    def _():
        st_ref[...] = jnp.zeros_like(st_ref)

    lb = lb_ref[...]
    one_m = 1.0 - lb
    ng = ng_ref[...]
    c = chunk

    def body(ci, carry):
        r0 = pl.multiple_of(ci * c, c)
        q = q_ref[0, pl.ds(r0, c), :]
        hf = f_ref[0, pl.ds(r0, c), :]
        v = i_ref[0, pl.ds(r0, c), :]
        hg = g_ref[0, pl.ds(r0, c), :]
        logf = jnp.log(lb + one_m * jax.nn.sigmoid(hf))
        k = one_m * jax.nn.sigmoid(-hf)
        lf_hi = logf.astype(BF16)
        lf_lo = (logf - lf_hi.astype(F32)).astype(BF16)
        g2 = jnp.dot(w_ref[...], jnp.concatenate([lf_hi, lf_lo], axis=1), preferred_element_type=F32)
        gsum = g2[:, :HG_HEAD_DIM] + g2[:, HG_HEAD_DIM:]
        cum = gsum[:c]
        scores = m_ref[0] * lax.dot_general(q.astype(BF16), k.astype(BF16), NT_DIMS, preferred_element_type=F32)
        for li in range(1, n_levels + 1):
            e = jnp.exp(gsum[li * c:(li + 1) * c])
            p = lax.dot_general((q * e).astype(BF16), (k * e).astype(BF16), NT_DIMS, preferred_element_type=F32)
            scores = scores + m_ref[li] * p
        st = st_ref[...]
        vb = v.astype(BF16)
        o = jnp.dot(scores.astype(BF16), vb, preferred_element_type=F32)
        o = o + lax.dot_general((q * jnp.exp(cum)).astype(BF16), st.astype(BF16), NT_DIMS,
                                preferred_element_type=F32)
        last = cum[c - 1:c, :]
        kd = (k * jnp.exp(last - cum)).astype(BF16)
        st_ref[...] = jnp.exp(last) * st + lax.dot_general(vb, kd, TN_DIMS, preferred_element_type=F32)
        o = o * lax.rsqrt(jnp.mean(o * o, axis=-1, keepdims=True) + RMS_EPS) * ng
        o_ref[0, pl.ds(r0, c), :] = (o * (hg * jax.nn.sigmoid(hg))).astype(o_ref.dtype)
        return carry

    lax.fori_loop(0, n_chunks, body, 0)

    @pl.when(t == pl.num_programs(2) - 1)
    def _():
        s_ref[0, 0] = st_ref[...].T


def _hgrn_prompt(h3, lb, ng, chunk=64, tb=512):
    b, t, _ = h3.shape
    tb = min(tb, t)
    chunk = min(chunk, tb)
    w_np, m_np = _hgrn_consts(chunk)
    n_levels = m_np.shape[0] - 1
    w = jnp.asarray(w_np, BF16)
    m = jnp.asarray(m_np, F32)
    col = lambda base: pl.BlockSpec((1, tb, HG_HEAD_DIM), lambda bi, j, ti, base=base: (bi, ti, base + j))
    kern = functools.partial(_hgrn_kernel, chunk=chunk, n_chunks=tb // chunk, n_levels=n_levels)
    return pl.pallas_call(
        kern,
        grid=(b, HG_HEADS, t // tb),
        in_specs=[col(COL_HQ), col(COL_HF), col(COL_HI), col(COL_HG),
                  pl.BlockSpec((1, HG_HEAD_DIM), lambda bi, j, ti: (0, j)),
                  pl.BlockSpec((1, HG_HEAD_DIM), lambda bi, j, ti: (0, 0)),
                  pl.BlockSpec(w.shape, lambda bi, j, ti: (0, 0)),
                  pl.BlockSpec(m.shape, lambda bi, j, ti: (0, 0, 0))],
        out_specs=[pl.BlockSpec((1, tb, HG_HEAD_DIM), lambda bi, j, ti: (bi, ti, j)),
                   pl.BlockSpec((1, 1, HG_HEAD_DIM, HG_HEAD_DIM), lambda bi, j, ti: (bi, j, 0, 0))],
        out_shape=[jax.ShapeDtypeStruct((b, t, HG_WIDTH), BF16),
                   jax.ShapeDtypeStruct((b, HG_HEADS, HG_HEAD_DIM, HG_HEAD_DIM), F32)],
        scratch_shapes=[pltpu.VMEM((HG_HEAD_DIM, HG_HEAD_DIM), F32)],
        compiler_params=_params(("parallel", "parallel", "arbitrary")),
        name="hgrn_prompt",
    )(h3, h3, h3, h3, lb, ng, w, m)


def _head_mask(shape, head):
    lane = lax.broadcasted_iota(jnp.int32, shape, 1)
    return (lane >= ATT_HEAD_DIM) if head % 2 else (lane < ATT_HEAD_DIM)


def _dsa_block(q_ref, qi_ref, iw_ref, kt_ref, vt_ref, kit_ref, o_ref, key_ref, bias_ref, *, s, tq, topk, q0):
    kit = kit_ref[0, :, :s].astype(BF16)
    kk = jnp.concatenate([kit, kit], axis=0)
    iw = iw_ref[0][:, IDX_DIM:IDX_DIM + IDX_HEADS] * (IDX_HEADS ** -0.5)
    score = None
    for h in range(IDX_HEADS):
        pair = qi_ref[0, :, (h // 2) * LANES:(h // 2 + 1) * LANES]
        qm = jnp.where(_head_mask(pair.shape, h), pair, 0.0).astype(BF16)
        d = jnp.dot(qm, kk, preferred_element_type=F32) * (IDX_DIM ** -0.5)
        term = iw[:, h:h + 1] * jnp.maximum(d, 0.0)
        score = term if score is None else score + term

    qpos = q0 + lax.broadcasted_iota(jnp.int32, (tq, s), 0)
    kpos = lax.broadcasted_iota(jnp.int32, (tq, s), 1)
    vis = kpos <= qpos
    key_ref[:, :s] = jnp.where(vis, _sort_key(score), jnp.int32(INT_MIN))

    def count_ge(cand):
        return jnp.sum(jnp.where(key_ref[:, :s] >= cand, 1.0, 0.0), axis=1, keepdims=True)

    kf = float(topk)
    th0 = jnp.where(count_ge(jnp.int32(0)) >= kf, jnp.int32(0), jnp.int32(INT_MIN))

    def search(j, th):
        cand = th | jnp.left_shift(jnp.int32(1), jnp.int32(30) - j)
        return jnp.where(count_ge(cand) >= kf, cand, th)

    th = lax.fori_loop(0, 31, search, th0)

    key = key_ref[:, :s]
    gt = (key > th) & vis
    tie = (key == th) & vis
    need = kf - jnp.sum(jnp.where(gt, 1.0, 0.0), axis=1, keepdims=True)
    n_tie = jnp.sum(jnp.where(tie, 1.0, 0.0), axis=1, keepdims=True)
    bias_ref[:, :s] = jnp.where(gt | tie, 0.0, -jnp.inf)

    @pl.when(jnp.max(n_tie - need) > 0.0)
    def _():
        key2 = key_ref[:, :s]
        tie2 = (key2 == th) & (kpos <= qpos)

        def tsearch(j, lim):
            cand = lim | jnp.left_shift(jnp.int32(1), jnp.int32(11) - j)
            cnt = jnp.sum(jnp.where(tie2 & (kpos < cand), 1.0, 0.0), axis=1, keepdims=True)
            return jnp.where(cnt <= need, cand, lim)

        lim = lax.fori_loop(0, 12, tsearch, jnp.zeros((tq, 1), jnp.int32))
        keep = ((key2 > th) & (kpos <= qpos)) | (tie2 & (kpos < lim))
        bias_ref[:, :s] = jnp.where(keep, 0.0, -jnp.inf)

    bias = bias_ref[:, :s]
    for hp in range(ATT_HEADS // 2):
        cols = slice(hp * LANES, (hp + 1) * LANES)
        qp = q_ref[0, :, cols]
        kp = kt_ref[0, cols, :s].astype(BF16)
        vp = vt_ref[0, cols, :s].astype(BF16)
        outs = []
        for e in range(2):
            qm = jnp.where(_head_mask(qp.shape, e), qp, 0.0).astype(BF16)
            sc = jnp.dot(qm, kp, preferred_element_type=F32) * (ATT_HEAD_DIM ** -0.5) + bias
            mx = jnp.max(sc, axis=1, keepdims=True)
            p = jnp.exp(sc - mx)
            l = jnp.sum(p, axis=1, keepdims=True)
            outs.append(lax.dot_general(p.astype(BF16), vp, NT_DIMS, preferred_element_type=F32) / l)
        o_ref[0, :, cols] = jnp.where(_head_mask(qp.shape, 0), outs[0], outs[1]).astype(o_ref.dtype)


def _dsa_kernel(q_ref, qi_ref, iw_ref, kt_ref, vt_ref, kit_ref, o_ref, key_ref, bias_ref,
                *, tq, topk, s_quant, t):
    i = pl.program_id(1)
    q0 = i * tq
    n_var = t // s_quant
    var = (q0 + tq - 1) // s_quant
    for vi in range(n_var):
        @pl.when(var == vi)
        def _(vi=vi):
            _dsa_block(q_ref, qi_ref, iw_ref, kt_ref, vt_ref, kit_ref, o_ref, key_ref, bias_ref,
                       s=(vi + 1) * s_quant, tq=tq, topk=topk, q0=q0)


def _dsa_prompt(h3, kt, vt, kit, tq=128, s_quant=512):
    b, t, _ = h3.shape
    tq = min(tq, t)
    s_quant = min(s_quant, t)
    topk = min(TOPK_MAX, t // 4)
    qblk = lambda c: pl.BlockSpec((1, tq, ATT_WIDTH), lambda bi, i, c=c: (bi, i, c))
    feat = lambda n: pl.BlockSpec((1, n, t), lambda bi, i: (bi, 0, 0))
    kern = functools.partial(_dsa_kernel, tq=tq, topk=topk, s_quant=s_quant, t=t)
    return pl.pallas_call(
        kern,
        grid=(b, t // tq),
        in_specs=[qblk(COL_AQ // 4), qblk(PCOL_IQ // 4),
                  pl.BlockSpec((1, tq, LANES), lambda bi, i: (bi, i, PCOL_IK)),
                  feat(ATT_WIDTH), feat(ATT_WIDTH), feat(IDX_DIM)],
        out_specs=pl.BlockSpec((1, tq, ATT_WIDTH), lambda bi, i: (bi, i, 0)),
        out_shape=jax.ShapeDtypeStruct((b, t, ATT_WIDTH), BF16),
        scratch_shapes=[pltpu.VMEM((tq, t), jnp.int32), pltpu.VMEM((tq, t), F32)],
        compiler_params=_params(("parallel", "arbitrary")),
        name="dsa_prompt",
    )(h3, h3, h3, kt, vt, kit)


def _outproj_kernel(x_ref, a_ref, b_ref, wa_ref, wb_ref, g_ref, beta_ref, o_ref, *, alpha):
    mix = jnp.dot(a_ref[...].astype(BF16), wa_ref[...], preferred_element_type=F32)
    mix = mix + jnp.dot(b_ref[...].astype(BF16), wb_ref[...], preferred_element_type=F32)
    o_ref[...] = _layer_norm(alpha * x_ref[...] + mix, g_ref[...], beta_ref[...])


def _outproj_ln(x2d, o_hg, o_att, w_out, g, beta, alpha, layer):
    m, d = x2d.shape
    tm = min(512, m)
    row = lambda w: pl.BlockSpec((tm, w), lambda i: (i, 0))
    vec = pl.BlockSpec((1, d), lambda i: (0, 0))
    return pl.pallas_call(
        functools.partial(_outproj_kernel, alpha=alpha),
        grid=(m // tm,),
        in_specs=[row(d), row(HG_WIDTH), row(ATT_WIDTH),
                  pl.BlockSpec((None, HG_WIDTH, d), lambda i: (layer, 0, 0)),
                  pl.BlockSpec((None, ATT_WIDTH, d), lambda i: (layer, 1, 0)),
                  vec, vec],
        out_specs=row(d),
        out_shape=jax.ShapeDtypeStruct((m, d), F32),
        compiler_params=_params(("parallel",)),
        name="outproj_ln",
    )(x2d, o_hg, o_att, w_out, w_out, g, beta)


def _ffn_kernel(x_ref, wu_ref, cw_ref, cb_ref, wd_ref, g_ref, beta_ref, o_ref, cs_ref, ubuf_ref,
                *, alpha, tm, d_ff, cw):
    ti = pl.program_id(1)
    halo = SUBLANES

    @pl.when(ti == 0)
    def _():
        ubuf_ref[0:halo, :] = jnp.zeros((halo, 2 * d_ff), F32)

    x = x_ref[0]
    xb = x.astype(BF16)
    acc = jnp.zeros((tm, x.shape[1]), F32)
    for j in range(d_ff // cw):
        act = []
        for base in (j * cw, d_ff + j * cw):
            cols = slice(base, base + cw)
            ubuf_ref[halo:halo + tm, cols] = jnp.dot(xb, wu_ref[:, cols], preferred_element_type=F32)
            c = cb_ref[:, cols]
            for tap in range(CONV_W):
                c = c + ubuf_ref[pl.ds(halo - (CONV_W - 1) + tap, tm), cols] * cw_ref[tap:tap + 1, cols]
            act.append(c)
        a, gt = act
        hmid = (a * (gt * jax.nn.sigmoid(gt))).astype(BF16)
        acc = acc + jnp.dot(hmid, wd_ref[j * cw:(j + 1) * cw, :], preferred_element_type=F32)
    o_ref[0] = _layer_norm(alpha * x + acc, g_ref[...], beta_ref[...])
    tail = ubuf_ref[tm:tm + halo, :]
    ubuf_ref[0:halo, :] = tail

    @pl.when(ti == pl.num_programs(1) - 1)
    def _():
        cs_ref[0] = tail[halo - (CONV_W - 1):, :]


def _ffn_prompt(x3, w_up, conv_w, conv_b, w_down, g, beta, alpha, layer, tm=256, cw=256):
    b, t, d = x3.shape
    d_ff = w_down.shape[1]
    tm = min(tm, t)
    kern = functools.partial(_ffn_kernel, alpha=alpha, tm=tm, d_ff=d_ff, cw=cw)
    const = lambda shape: pl.BlockSpec(shape, lambda bi, ti: (0,) * len(shape))
    return pl.pallas_call(
        kern,
        grid=(b, t // tm),
        in_specs=[pl.BlockSpec((1, tm, d), lambda bi, ti: (bi, ti, 0)),
                  _layer_spec(w_up, layer), const(conv_w.shape), const(conv_b.shape), _layer_spec(w_down, layer),
                  const(g.shape), const(beta.shape)],
        out_specs=[pl.BlockSpec((1, tm, d), lambda bi, ti: (bi, ti, 0)),
                   pl.BlockSpec((1, CONV_W - 1, 2 * d_ff), lambda bi, ti: (bi, 0, 0))],
        out_shape=[jax.ShapeDtypeStruct((b, t, d), F32),
                   jax.ShapeDtypeStruct((b, CONV_W - 1, 2 * d_ff), F32)],
        scratch_shapes=[pltpu.VMEM((tm + SUBLANES, 2 * d_ff), F32)],
        compiler_params=_params(("parallel", "arbitrary")),
        name="ffn_prompt",
    )(x3, w_up, conv_w, conv_b, w_down, g, beta)


def _hgrn_step_kernel(q_ref, f_ref, i_ref, g_ref, lb_ref, ng_ref, s_ref, o_ref, so_ref, obuf_ref, *, nb):
    lb = lb_ref[...]
    one_m = 1.0 - lb
    hf = f_ref[...]
    f = lb + one_m * jax.nn.sigmoid(hf)
    k = one_m * jax.nn.sigmoid(-hf)
    v = i_ref[...]
    pad = jnp.zeros((HG_HEAD_DIM - nb, HG_HEAD_DIM), F32)
    col = lambda a: jnp.concatenate([a, pad], axis=0).T
    f_t, k_t, q_t = col(f), col(k), col(q_ref[...])
    for b in range(nb):
        s_new = f_t[:, b:b + 1] * s_ref[b, 0] + k_t[:, b:b + 1] * v[b:b + 1, :]
        so_ref[b, 0] = s_new
        obuf_ref[b:b + 1, :] = jnp.sum(q_t[:, b:b + 1] * s_new, axis=0, keepdims=True)
    o = obuf_ref[...]
    o = o * lax.rsqrt(jnp.mean(o * o, axis=-1, keepdims=True) + RMS_EPS) * ng_ref[...]
    hg = g_ref[...]
    o_ref[...] = (o * (hg * jax.nn.sigmoid(hg))).astype(o_ref.dtype)


def _hgrn_step(h2, lb, ng, state, layer):
    nb = h2.shape[0]
    col = lambda base: pl.BlockSpec((nb, HG_HEAD_DIM), lambda j, base=base: (0, base + j))
    sblk = pl.BlockSpec((nb, 1, HG_HEAD_DIM, HG_HEAD_DIM), lambda j: (0, j, 0, 0))
    return pl.pallas_call(
        functools.partial(_hgrn_step_kernel, nb=nb),
        grid=(HG_HEADS,),
        in_specs=[col(COL_HQ), col(COL_HF), col(COL_HI), col(COL_HG),
                  pl.BlockSpec((1, HG_HEAD_DIM), lambda j: (0, j)),
                  pl.BlockSpec((1, HG_HEAD_DIM), lambda j: (0, 0)),
                  pl.BlockSpec((None, nb, 1, HG_HEAD_DIM, HG_HEAD_DIM), lambda j: (layer, 0, j, 0, 0))],
        out_specs=[pl.BlockSpec((nb, HG_HEAD_DIM), lambda j: (0, j)), sblk],
        out_shape=[jax.ShapeDtypeStruct((nb, HG_WIDTH), BF16),
                   jax.ShapeDtypeStruct(state.shape[1:], F32)],
        scratch_shapes=[pltpu.VMEM((nb, HG_HEAD_DIM), F32)],
        compiler_params=_params(("parallel",)),
        name="hgrn_step",
    )(h2, h2, h2, h2, lb, ng, state)


def _sidx_kernel(pt_ref, qi_ref, kin_ref, iw_ref, kidx_hbm, bias_ref, bself_ref, kbuf, sem, *, n_pages, topk, layer):
    b = pl.program_id(0)
    nb = pl.num_programs(0)

    def copies(bb, slot):
        return [pltpu.make_async_copy(kidx_hbm.at[layer, pt_ref[bb * n_pages + p]], kbuf.at[slot, p],
                                      sem.at[slot])
                for p in range(n_pages)]

    @pl.when(b == 0)
    def _():
        for cp in copies(0, 0):
            cp.start()

    @pl.when(b + 1 < nb)
    def _():
        for cp in copies(b + 1, (b + 1) % 2):
            cp.start()

    slot = b % 2
    for cp in copies(b, slot):
        cp.wait()

    qi = qi_ref[0]
    kin = kin_ref[0]
    iw = iw_ref[0] * (IDX_HEADS ** -0.5)
    pages = kbuf[slot]
    sc = jnp.zeros((n_pages, PAGE_SIZE), F32)
    s_self = jnp.zeros((1, 1), F32)
    for h in range(IDX_HEADS):
        row = qi[h:h + 1, :]
        d = jnp.sum(pages * _row_to_col(row)[None], axis=1) * (IDX_DIM ** -0.5)
        w_h = iw[:, h:h + 1]
        sc = sc + w_h * jnp.maximum(d, 0.0)
        d_self = jnp.sum(row * kin, axis=1, keepdims=True) * (IDX_DIM ** -0.5)
        s_self = s_self + w_h * jnp.maximum(d_self, 0.0)

    key = _sort_key(sc)
    key_self = _sort_key(s_self)
    pos = (PAGE_SIZE * lax.broadcasted_iota(jnp.int32, key.shape, 0)
           + lax.broadcasted_iota(jnp.int32, key.shape, 1))
    pos_self = n_pages * PAGE_SIZE
    kf = float(topk)

    def total(mask, mask_self):
        return jnp.sum(jnp.where(mask, 1.0, 0.0), keepdims=True) + jnp.where(mask_self, 1.0, 0.0)

    def search(j, th):
        cand = th | jnp.left_shift(jnp.int32(1), jnp.int32(30) - j)
        return jnp.where(total(key >= cand, key_self >= cand) >= kf, cand, th)

    zero = jnp.zeros((1, 1), jnp.int32)
    th0 = jnp.where(total(key >= zero, key_self >= zero) >= kf, zero, jnp.int32(INT_MIN))
    th = lax.fori_loop(0, 31, search, th0)

    gt, gt_self = key > th, key_self > th
    tie, tie_self = key == th, key_self == th
    need = kf - total(gt, gt_self)

    nbits = int(pos_self).bit_length() + 1

    def tsearch(j, lim):
        cand = lim | jnp.left_shift(jnp.int32(1), jnp.int32(nbits - 1) - j)
        return jnp.where(total(tie & (pos < cand), tie_self & (pos_self < cand)) <= need, cand, lim)

    lim = lax.fori_loop(0, nbits, tsearch, zero)
    bias_ref[0] = jnp.where(gt | (tie & (pos < lim)), 0.0, -jnp.inf)
    keep_self = gt_self | (tie_self & (pos_self < lim))
    bself_ref[0] = jnp.broadcast_to(jnp.where(keep_self, 0.0, -jnp.inf), (1, LANES))


def _sample_index(qi3, kin3, iw3, kidx_l, page_table_flat, n_pages, topk, layer):
    nb = qi3.shape[0]
    kern = functools.partial(_sidx_kernel, n_pages=n_pages, topk=topk, layer=layer)
    per_b = lambda a: pl.BlockSpec((1,) + a.shape[1:], lambda b, pt: (b, 0, 0))
    grid_spec = pltpu.PrefetchScalarGridSpec(
        num_scalar_prefetch=1,
        grid=(nb,),
        in_specs=[per_b(qi3), per_b(kin3), per_b(iw3), pl.BlockSpec(memory_space=pl.ANY)],
        out_specs=[pl.BlockSpec((1, n_pages, PAGE_SIZE), lambda b, pt: (b, 0, 0)),
                   pl.BlockSpec((1, 1, LANES), lambda b, pt: (b, 0, 0))],
        scratch_shapes=[pltpu.VMEM((2, n_pages, IDX_DIM, PAGE_SIZE), F32),
                        pltpu.SemaphoreType.DMA((2,))],
    )
    return pl.pallas_call(
        kern,
        grid_spec=grid_spec,
        out_shape=[jax.ShapeDtypeStruct((nb, n_pages, PAGE_SIZE), F32),
                   jax.ShapeDtypeStruct((nb, 1, LANES), F32)],
        compiler_params=_params(("arbitrary",)),
        name="sample_index",
    )(page_table_flat, qi3, kin3, iw3, kidx_l)


def _satt_kernel(pt_ref, q_ref, kn_ref, vn_ref, bias_ref, bself_ref, k_hbm, v_hbm, o_ref,
                 buf, prob_ref, acc_ref, l_ref, oself_ref, sem, *, n_pages, cp_pages, layer):
    b = pl.program_id(0)
    s = pl.program_id(1)
    ns = pl.num_programs(1)
    nc = n_pages // cp_pages
    step = b * ns + s
    n_steps = pl.num_programs(0) * ns
    scale = ATT_HEAD_DIM ** -0.5

    def copies(src, bb, chunk, slot):
        return [pltpu.make_async_copy(src.at[layer, pt_ref[bb * n_pages + chunk * cp_pages + p]], buf.at[slot, p],
                                      sem.at[slot]) for p in range(cp_pages)]

    def for_stage(st, slot, fn):
        bb = st // ns
        ss = st % ns

        @pl.when(ss < nc)
        def _():
            for cp in copies(k_hbm, bb, ss, slot):
                fn(cp)

        @pl.when(ss >= nc)
        def _():
            for cp in copies(v_hbm, bb, ss - nc, slot):
                fn(cp)

    @pl.when(step == 0)
    def _():
        for_stage(0, 0, lambda cp: cp.start())

    @pl.when(step + 1 < n_steps)
    def _():
        for_stage(step + 1, (step + 1) % 2, lambda cp: cp.start())

    slot = step % 2
    for_stage(step, slot, lambda cp: cp.wait())

    q = q_ref[0]

    @pl.when(s < nc)
    def _():
        pages = buf[slot]
        r0 = pl.multiple_of(s * cp_pages, cp_pages)
        bias = bias_ref[0, pl.ds(r0, cp_pages), :]
        for h in range(ATT_HEADS):
            qc = _row_to_col(q[h:h + 1, :])
            prob_ref[h, pl.ds(r0, cp_pages), :] = jnp.sum(pages[:, h] * qc[None], axis=1) * scale + bias

    @pl.when(s == nc)
    def _():
        lg = prob_ref[...]
        s_self = jnp.sum(q * kn_ref[0], axis=1, keepdims=True) * scale + bself_ref[0][:, :1]
        mx = jnp.maximum(jnp.max(jnp.max(lg, axis=1), axis=1, keepdims=True), s_self)
        p = jnp.exp(lg - mx[:, :, None])
        p_self = jnp.exp(s_self - mx)
        prob_ref[...] = p
        l_ref[...] = jnp.sum(jnp.sum(p, axis=1), axis=1, keepdims=True) + p_self
        oself_ref[...] = p_self * vn_ref[0]
        acc_ref[...] = jnp.zeros_like(acc_ref)

    @pl.when(s >= nc)
    def _():
        pages = buf[slot]
        r0 = pl.multiple_of((s - nc) * cp_pages, cp_pages)
        for h in range(ATT_HEADS):
            p = prob_ref[h, pl.ds(r0, cp_pages), :]
            acc_ref[h] += jnp.sum(pages[:, h] * p[:, None, :], axis=0)

    @pl.when(s == ns - 1)
    def _():
        for h in range(ATT_HEADS):
            row = _col_to_row(jnp.sum(acc_ref[h], axis=1, keepdims=True))
            o_ref[0, h:h + 1, :] = (row + oself_ref[h:h + 1, :]) / l_ref[h:h + 1, :]


def _sample_attend(q3, kn3, vn3, bias, bself, k_l, v_l, page_table_flat, n_pages, layer, cp_pages=8):
    nb = q3.shape[0]
    cp_pages = min(cp_pages, n_pages)
    nc = n_pages // cp_pages
    kern = functools.partial(_satt_kernel, n_pages=n_pages, cp_pages=cp_pages, layer=layer)
    per_b = lambda a: pl.BlockSpec((1,) + a.shape[1:], lambda b, s, pt: (b, 0, 0))
    grid_spec = pltpu.PrefetchScalarGridSpec(
        num_scalar_prefetch=1,
        grid=(nb, 2 * nc),
        in_specs=[per_b(q3), per_b(kn3), per_b(vn3), per_b(bias), per_b(bself),
                  pl.BlockSpec(memory_space=pl.ANY), pl.BlockSpec(memory_space=pl.ANY)],
        out_specs=per_b(q3),
        scratch_shapes=[pltpu.VMEM((2, cp_pages, ATT_HEADS, ATT_HEAD_DIM, PAGE_SIZE), F32),
                        pltpu.VMEM((ATT_HEADS, n_pages, PAGE_SIZE), F32),
                        pltpu.VMEM((ATT_HEADS, ATT_HEAD_DIM, PAGE_SIZE), F32),
                        pltpu.VMEM((ATT_HEADS, 1), F32),
                        pltpu.VMEM((ATT_HEADS, ATT_HEAD_DIM), F32),
                        pltpu.SemaphoreType.DMA((2,))],
    )
    return pl.pallas_call(
        kern,
        grid_spec=grid_spec,
        out_shape=jax.ShapeDtypeStruct(q3.shape, F32),
        compiler_params=_params(("arbitrary", "arbitrary")),
        name="sample_attend",
    )(page_table_flat, q3, kn3, vn3, bias, bself, k_l, v_l)


def _ffn_step_kernel(x_ref, wu_ref, cw_ref, cb_ref, wd_ref, g_ref, beta_ref, buf_ref, o_ref, nbuf_ref,
                     *, alpha, d_ff):
    x = x_ref[...]
    u = jnp.dot(x.astype(BF16), wu_ref[...], preferred_element_type=F32)
    prev = [buf_ref[:, j, :] for j in range(CONV_W - 1)]
    ext = prev + [u]
    c = cb_ref[...]
    for tap in range(CONV_W):
        c = c + ext[tap] * cw_ref[tap:tap + 1, :]
    a, gt = c[:, :d_ff], c[:, d_ff:]
    hmid = (a * (gt * jax.nn.sigmoid(gt))).astype(BF16)
    ffn = jnp.dot(hmid, wd_ref[...], preferred_element_type=F32)
    o_ref[...] = _layer_norm(alpha * x + ffn, g_ref[...], beta_ref[...])
    for j in range(CONV_W - 1):
        nbuf_ref[:, j, :] = ext[j + 1]


def _ffn_step(x2d, w_up, conv_w, conv_b, w_down, g, beta, buf, alpha, layer):
    nb, d = x2d.shape
    d_ff = w_down.shape[1]
    args = (x2d, w_up, conv_w, conv_b, w_down, g, beta, buf)
    full = lambda a: pl.BlockSpec(a.shape, lambda i: (0,) * a.ndim)
    return pl.pallas_call(
        functools.partial(_ffn_step_kernel, alpha=alpha, d_ff=d_ff),
        grid=(1,),
        in_specs=[full(x2d), _layer_spec(w_up, layer), full(conv_w), full(conv_b), _layer_spec(w_down, layer),
                  full(g), full(beta), _layer_spec(buf, layer)],
        out_specs=[pl.BlockSpec((nb, d), lambda i: (0, 0)),
                   pl.BlockSpec(buf.shape[1:], lambda i: (0, 0, 0))],
        out_shape=[jax.ShapeDtypeStruct((nb, d), F32), jax.ShapeDtypeStruct(buf.shape[1:], F32)],
        compiler_params=_params(("arbitrary",)),
        name="ffn_step",
    )(*args)


def _lower_bounds(lb_param):
    p = jax.nn.softmax(lb_param.astype(F32), axis=0)
    return jnp.cumsum(p, axis=0) - p[0:1]


def kernel(x_prompt, x_sample, cache_k, cache_v, cache_kidx, state_hgrn, state_conv, page_table, w_in, lb_param,
           hg_norm_g, w_out, ln1_g, ln1_b, w_up, conv_w, conv_b, w_down, ln2_g, ln2_b):
    depth = w_in.shape[0]
    b, t, d = x_prompt.shape
    nb = x_sample.shape[0]
    assert x_sample.shape[1] == 1
    n_pages = page_table.shape[1]
    alpha = (2 * depth) ** 0.25

    lbs = _lower_bounds(lb_param)
    pad = jnp.zeros(w_in.shape[:-1] + (N_FULL - w_in.shape[-1],), w_in.dtype)
    w_full = jnp.concatenate([w_in, pad], axis=-1).astype(BF16)
    w_in_t = jnp.swapaxes(w_in, 1, 2)
    w_feat = jnp.concatenate([w_in_t[:, COL_AK * LANES:COL_IQ * LANES],
                              w_in_t[:, COL_IK * LANES:COL_IK * LANES + IDX_DIM]], axis=1).astype(BF16)
    w_out_b, w_up_b, w_down_b = w_out.astype(BF16), w_up.astype(BF16), w_down.astype(BF16)
    pt_flat = page_table.reshape(-1).astype(jnp.int32)
    kidx_v = jnp.swapaxes(cache_kidx, 2, 3)
    k_v = jnp.transpose(cache_k, (0, 1, 3, 4, 2))
    v_v = jnp.transpose(cache_v, (0, 1, 3, 4, 2))
    topk_s = min(TOPK_MAX, (n_pages * PAGE_SIZE + 1) // 4)

    xp = x_prompt
    xs = x_sample.reshape(nb, d)
    outs = {n: [] for n in ("kp", "vp", "kip", "hgp", "cbp", "ks", "vs", "kis", "hgs", "cbs")}
    seg = lambda h, c0, w: h[..., c0 * LANES:c0 * LANES + w]

    for l in range(depth):
        vec = lambda a: a[l][None, :]
        lb_l, ng_l = vec(lbs), vec(hg_norm_g)

        h3, kt, vt, kit = _inproj_prompt(xp, w_full, w_feat, l)
        o_hg, s_new = _hgrn_prompt(h3, lb_l, ng_l)
        o_att = _dsa_prompt(h3, kt, vt, kit)
        x1 = _outproj_ln(xp.reshape(b * t, d), o_hg.reshape(b * t, HG_WIDTH), o_att.reshape(b * t, ATT_WIDTH),
                         w_out_b, vec(ln1_g), vec(ln1_b), alpha, l)
        xp, cb_new = _ffn_prompt(x1.reshape(b, t, d), w_up_b, conv_w[l], vec(conv_b), w_down_b,
                                 vec(ln2_g), vec(ln2_b), alpha, l)
        outs["kp"].append(kt)
        outs["vp"].append(vt)
        outs["kip"].append(kit)
        outs["hgp"].append(s_new)
        outs["cbp"].append(cb_new)

        hs = _inproj_sample(xs, w_full, l)
        o_hg_s, s_new_s = _hgrn_step(hs, lb_l, ng_l, state_hgrn, l)
        heads = lambda c0: seg(hs, c0, ATT_WIDTH).reshape(nb, ATT_HEADS, ATT_HEAD_DIM)
        kin3 = seg(hs, COL_IK, IDX_DIM).reshape(nb, 1, IDX_DIM)
        iw3 = hs[:, COL_IK * LANES + IDX_DIM:COL_IK * LANES + IDX_DIM + IDX_HEADS].reshape(nb, 1, IDX_HEADS)
        bias, bself = _sample_index(heads(COL_IQ), kin3, iw3, kidx_v, pt_flat, n_pages, topk_s, l)
        kn3, vn3 = heads(COL_AK), heads(COL_AV)
        o_att_s = _sample_attend(heads(COL_AQ), kn3, vn3, bias, bself, k_v, v_v, pt_flat, n_pages, l)
        x1s = _outproj_ln(xs, o_hg_s, o_att_s.reshape(nb, ATT_WIDTH), w_out_b, vec(ln1_g), vec(ln1_b), alpha, l)
        xs, cb_new_s = _ffn_step(x1s, w_up_b, conv_w[l], vec(conv_b), w_down_b, vec(ln2_g), vec(ln2_b),
                                 state_conv, alpha, l)
        outs["ks"].append(kn3.reshape(nb, 1, ATT_HEADS, ATT_HEAD_DIM))
        outs["vs"].append(vn3.reshape(nb, 1, ATT_HEADS, ATT_HEAD_DIM))
        outs["kis"].append(kin3)
        outs["hgs"].append(s_new_s)
        outs["cbs"].append(cb_new_s)

    st = {n: jnp.stack(v) for n, v in outs.items()}
    k_p = jnp.transpose(st["kp"].reshape(depth, b, ATT_HEADS, ATT_HEAD_DIM, t), (0, 1, 4, 2, 3))
    v_p = jnp.transpose(st["vp"].reshape(depth, b, ATT_HEADS, ATT_HEAD_DIM, t), (0, 1, 4, 2, 3))
    ki_p = jnp.swapaxes(st["kip"], 2, 3)
    return (xp, xs.reshape(nb, 1, d), k_p, v_p, ki_p, st["hgp"], st["cbp"],
            st["ks"], st["vs"], st["kis"], st["hgs"], st["cbs"])
```

```python
import functools

import numpy as np
import jax
import jax.numpy as jnp
from jax import lax
from jax.experimental import pallas as pl
from jax.experimental.pallas import tpu as pltpu

F32 = jnp.float32
BF16 = jnp.bfloat16

HG_HEADS = 4
HG_HEAD_DIM = 128
HG_WIDTH = HG_HEADS * HG_HEAD_DIM
ATT_HEADS = 8
ATT_HEAD_DIM = 64
ATT_WIDTH = ATT_HEADS * ATT_HEAD_DIM
IDX_HEADS = 8
IDX_DIM = 64
TOPK_MAX = 256
CONV_W = 3
PAGE_SIZE = 128
LN_EPS = 1e-5
RMS_EPS = 1e-6
LOG2_E = 1.4426950408889634

LANES = 128
SUBLANES = 8
VMEM_LIMIT = 56 * 1024 * 1024

COL_HQ, COL_HF, COL_HI, COL_HG = 0, 4, 8, 12
COL_AQ, COL_AK, COL_AV, COL_IQ, COL_IK = 16, 20, 24, 28, 32
N_FULL = 33 * LANES
PCOL_IQ, PCOL_IK = 20, 24
N_TOK = 25 * LANES
N_FEAT = 2 * ATT_WIDTH + IDX_DIM
INT_MIN = -2 ** 31
KEY_NEG_INF = INT_MIN + 0x7FFFFF

NT_DIMS = (((1,), (1,)), ((), ()))
TN_DIMS = (((0,), (0,)), ((), ()))


def _params(sem, vmem=VMEM_LIMIT):
    return pltpu.CompilerParams(dimension_semantics=sem, vmem_limit_bytes=vmem)


def _layer_norm(y, g, b):
    mu = jnp.mean(y, axis=-1, keepdims=True)
    d = y - mu
    var = jnp.mean(d * d, axis=-1, keepdims=True)
    return d * lax.rsqrt(var + LN_EPS) * g + b


def _layer_spec(arr, layer):
    zeros = (0,) * (arr.ndim - 1)
    return pl.BlockSpec((None,) + arr.shape[1:], lambda *_: (layer,) + zeros, pipeline_mode=pl.Buffered(1))


def _eye(n):
    return lax.broadcasted_iota(jnp.int32, (n, n), 0) == lax.broadcasted_iota(jnp.int32, (n, n), 1)


def _row_to_col(row):
    n = row.shape[1]
    return jnp.sum(jnp.where(_eye(n), jnp.broadcast_to(row, (n, n)), 0.0), axis=1, keepdims=True)


def _col_to_row(col):
    n = col.shape[0]
    return jnp.sum(jnp.where(_eye(n), jnp.broadcast_to(col, (n, n)), 0.0), axis=0, keepdims=True)


def _key_to_float(key):
    bits = jnp.where(key < 0, key ^ jnp.int32(0x7FFFFFFF), key)
    return jnp.where(key <= jnp.int32(KEY_NEG_INF), -jnp.inf, pltpu.bitcast(bits, F32))


def _count(mask):
    return jnp.sum(jnp.where(mask, 1.0, 0.0), axis=1, keepdims=True)


def _topk_bias(score_ref, bias_ref, s, vis_fn, topk, pos_bits):
    rows = score_ref.shape[0]
    kf = float(topk)
    assert rows % (2 * SUBLANES) == 0 and s % LANES == 0
    rg = rows // 2

    def partial_count(g, key):
        cand = _key_to_float(key)
        acc = None
        for c0 in range(0, s, LANES):
            hit = jnp.where(score_ref[g * rg:(g + 1) * rg, c0:c0 + LANES] >= cand, 1.0, 0.0)
            acc = hit if acc is None else acc + hit
        return acc

    def decide(part, cand, th):
        return jnp.where(jnp.sum(part, axis=1, keepdims=True) >= kf, cand, th)

    bit = lambda j: jnp.left_shift(jnp.int32(1), jnp.int32(30) - j)
    zero = jnp.zeros((rg, 1), jnp.int32)
    low = jnp.full((rg, 1), INT_MIN, jnp.int32)
    th_a = decide(partial_count(0, zero), zero, low)
    th_b = decide(partial_count(1, zero), zero, low)

    def search(j, carry):
        th_a, th_b, part_b = carry
        cand_a = th_a | bit(j)
        part_a = partial_count(0, cand_a)
        th_b = decide(part_b, th_b | bit(j), th_b)
        part_b = partial_count(1, th_b | bit(j + 1))
        return decide(part_a, cand_a, th_a), th_b, part_b

    th_a, th_b, part_b = lax.fori_loop(0, 30, search, (th_a, th_b, partial_count(1, th_b | bit(0))))
    th_a = decide(partial_count(0, th_a | bit(30)), th_a | bit(30), th_a)
    th_b = decide(part_b, th_b | bit(30), th_b)
    th = _key_to_float(jnp.concatenate([th_a, th_b], axis=0))
    score = score_ref[:, :s]
    vis, _ = vis_fn()
    gt = score > th
    tie = (score == th) & vis
    need = kf - _count(gt)
    bias_ref[:, :s] = jnp.where(gt | tie, 0.0, -jnp.inf)

    @pl.when(jnp.max(_count(tie) - need) > 0.0)
    def _():
        sc2 = score_ref[:, :s]
        vis2, kpos = vis_fn()
        tie2 = (sc2 == th) & vis2

        def tsearch(j, lim):
            cand = lim | jnp.left_shift(jnp.int32(1), jnp.int32(pos_bits - 1) - j)
            return jnp.where(_count(tie2 & (kpos < cand)) <= need, cand, lim)

        lim = lax.fori_loop(0, pos_bits, tsearch, jnp.zeros((rows, 1), jnp.int32))
        bias_ref[:, :s] = jnp.where((sc2 > th) | (tie2 & (kpos < lim)), 0.0, -jnp.inf)


def _inproj_prompt_kernel(x_ref, w_ref, wt_ref, h_ref, kt_ref, vt_ref, kit_ref):
    xb = x_ref[0].astype(BF16)
    n_a = COL_AK * LANES
    h_ref[0, :, :n_a] = jnp.dot(xb, w_ref[:, :n_a], preferred_element_type=F32)
    h_ref[0, :, n_a:] = jnp.dot(xb, w_ref[:, COL_IQ * LANES:], preferred_element_type=F32)
    ft = lax.dot_general(wt_ref[...], xb, NT_DIMS, preferred_element_type=F32)
    kt_ref[0] = ft[:ATT_WIDTH]
    vt_ref[0] = ft[ATT_WIDTH:2 * ATT_WIDTH]
    kit_ref[0] = ft[2 * ATT_WIDTH:]


def _inproj_prompt(x3, w_full, w_feat, layer, tm=256):
    b, t, d = x3.shape
    tm = min(tm, t)
    feat = lambda n: pl.BlockSpec((1, n, tm), lambda bi, ti: (bi, 0, ti))
    return pl.pallas_call(
        _inproj_prompt_kernel,
        grid=(b, t // tm),
        in_specs=[pl.BlockSpec((1, tm, d), lambda bi, ti: (bi, ti, 0)),
                  _layer_spec(w_full, layer), _layer_spec(w_feat, layer)],
        out_specs=[pl.BlockSpec((1, tm, N_TOK), lambda bi, ti: (bi, ti, 0)),
                   feat(ATT_WIDTH), feat(ATT_WIDTH), feat(IDX_DIM)],
        out_shape=[jax.ShapeDtypeStruct((b, t, N_TOK), F32),
                   jax.ShapeDtypeStruct((b, ATT_WIDTH, t), F32),
                   jax.ShapeDtypeStruct((b, ATT_WIDTH, t), F32),
                   jax.ShapeDtypeStruct((b, IDX_DIM, t), F32)],
        compiler_params=_params(("parallel", "parallel")),
        name="inproj_prompt",
    )(x3, w_full, w_feat)


def _inproj_kernel(x_ref, w_ref, h_ref):
    h_ref[...] = jnp.dot(x_ref[...].astype(BF16), w_ref[...], preferred_element_type=F32)


def _inproj_sample(x2d, w_full, layer):
    m = x2d.shape[0]
    return pl.pallas_call(
        _inproj_kernel,
        grid=(1,),
        in_specs=[pl.BlockSpec(x2d.shape, lambda i: (0, 0)), _layer_spec(w_full, layer)],
        out_specs=pl.BlockSpec((m, N_FULL), lambda i: (0, 0)),
        out_shape=jax.ShapeDtypeStruct((m, N_FULL), F32),
        compiler_params=_params(("arbitrary",)),
        name="inproj_sample",
    )(x2d, w_full)


def _hgrn_consts(c):
    nl = int(np.log2(c))
    w = np.zeros((nl + 1, c, c), np.float32)
    m = np.zeros((nl + 1, c, c), np.float32)
    w[0] = np.tril(np.ones((c, c), np.float32))
    m[0] = np.eye(c, dtype=np.float32)
    r = np.arange(c)
    for li in range(1, nl + 1):
        b = 2 ** li
        hb = b // 2
        mid = (r // b) * b + hb
        for t in range(c):
            if t % b >= hb:
                w[li, t, mid[t]:t + 1] = 1.0
            else:
                w[li, t, t + 1:mid[t]] = 1.0
        same = (r[:, None] // b) == (r[None, :] // b)
        m[li] = (same & ((r[:, None] % b) >= hb) & ((r[None, :] % b) < hb)).astype(np.float32)
    return w.reshape((nl + 1) * c, c), m


def _hgrn_kernel(q_ref, f_ref, i_ref, g_ref, lb_ref, ng_ref, w_ref, m_ref, o_ref, s_ref, st_ref,
                 *, chunk, n_chunks, n_levels):
    t = pl.program_id(1)

    @pl.when(t == 0)
    def _():
        st_ref[...] = jnp.zeros_like(st_ref)

    ng = ng_ref[...]
    c = chunk

    heads = range(HG_HEADS)
    cols = [slice(j * HG_HEAD_DIM, (j + 1) * HG_HEAD_DIM) for j in heads]

    def body(ci, carry):
        r0 = pl.multiple_of(ci * c, c)
        rows = pl.ds(r0, c)
        q, k, gsum = [], [], []
        for j in heads:
            lb = lb_ref[:, cols[j]]
            one_m = 1.0 - lb
            hf = f_ref[0, rows, cols[j]]
            logf = jnp.log(lb + one_m * jax.nn.sigmoid(hf))
            q.append(q_ref[0, rows, cols[j]])
            k.append(one_m * jax.nn.sigmoid(-hf))
            lf_hi = logf.astype(BF16)
            lf_lo = (logf - lf_hi.astype(F32)).astype(BF16)
            g2 = jnp.dot(w_ref[...], jnp.concatenate([lf_hi, lf_lo], axis=1), preferred_element_type=F32)
            gsum.append(g2[:, :HG_HEAD_DIM] + g2[:, HG_HEAD_DIM:])
        scores = [m_ref[0] * lax.dot_general(q[j].astype(BF16), k[j].astype(BF16), NT_DIMS,
                                             preferred_element_type=F32) for j in heads]
        for li in range(1, n_levels + 1):
            for j in heads:
                e = jnp.exp(gsum[j][li * c:(li + 1) * c])
                p = lax.dot_general((q[j] * e).astype(BF16), (k[j] * e).astype(BF16), NT_DIMS,
                                    preferred_element_type=F32)
                scores[j] = scores[j] + m_ref[li] * p
        o = []
        for j in heads:
            cum = gsum[j][:c]
            st = st_ref[j]
            vb = i_ref[0, rows, cols[j]].astype(BF16)
            oj = jnp.dot(scores[j].astype(BF16), vb, preferred_element_type=F32)
            oj = oj + lax.dot_general((q[j] * jnp.exp(cum)).astype(BF16), st.astype(BF16), NT_DIMS,
                                      preferred_element_type=F32)
            last = cum[c - 1:c, :]
            kd = (k[j] * jnp.exp(last - cum)).astype(BF16)
            st_ref[j] = jnp.exp(last) * st + lax.dot_general(vb, kd, TN_DIMS, preferred_element_type=F32)
            o.append(oj)
        for j in heads:
            hg = g_ref[0, rows, cols[j]]
            oj = o[j] * lax.rsqrt(jnp.mean(o[j] * o[j], axis=-1, keepdims=True) + RMS_EPS) * ng
            o_ref[0, rows, cols[j]] = (oj * (hg * jax.nn.sigmoid(hg))).astype(o_ref.dtype)
        return carry

    lax.fori_loop(0, n_chunks, body, 0)

    @pl.when(t == pl.num_programs(1) - 1)
    def _():
        for j in range(HG_HEADS):
            s_ref[0, j] = st_ref[j].T


def _hgrn_prompt(h3, lb, ng, chunk=64, tb=512):
    b, t, _ = h3.shape
    tb = min(tb, t)
    chunk = min(chunk, tb)
    w_np, m_np = _hgrn_consts(chunk)
    n_levels = m_np.shape[0] - 1
    w = jnp.asarray(w_np, BF16)
    m = jnp.asarray(m_np, F32)
    seg = lambda base: pl.BlockSpec((1, tb, HG_WIDTH), lambda bi, ti, base=base: (bi, ti, base // HG_HEADS))
    kern = functools.partial(_hgrn_kernel, chunk=chunk, n_chunks=tb // chunk, n_levels=n_levels)
    return pl.pallas_call(
        kern,
        grid=(b, t // tb),
        in_specs=[seg(COL_HQ), seg(COL_HF), seg(COL_HI), seg(COL_HG),
                  pl.BlockSpec((1, HG_WIDTH), lambda bi, ti: (0, 0)),
                  pl.BlockSpec((1, HG_HEAD_DIM), lambda bi, ti: (0, 0)),
                  pl.BlockSpec(w.shape, lambda bi, ti: (0, 0)),
                  pl.BlockSpec(m.shape, lambda bi, ti: (0, 0, 0))],
        out_specs=[pl.BlockSpec((1, tb, HG_WIDTH), lambda bi, ti: (bi, ti, 0)),
                   pl.BlockSpec((1, HG_HEADS, HG_HEAD_DIM, HG_HEAD_DIM), lambda bi, ti: (bi, 0, 0, 0))],
        out_shape=[jax.ShapeDtypeStruct((b, t, HG_WIDTH), BF16),
                   jax.ShapeDtypeStruct((b, HG_HEADS, HG_HEAD_DIM, HG_HEAD_DIM), F32)],
        scratch_shapes=[pltpu.VMEM((HG_HEADS, HG_HEAD_DIM, HG_HEAD_DIM), F32)],
        compiler_params=_params(("parallel", "arbitrary")),
        name="hgrn_prompt",
    )(h3, h3, h3, h3, lb, ng, w, m)


def _head_mask(shape, head):
    lane = lax.broadcasted_iota(jnp.int32, shape, 1)
    return (lane >= ATT_HEAD_DIM) if head % 2 else (lane < ATT_HEAD_DIM)


def _dsa_block(q_ref, qi_ref, iw_ref, kt_ref, vt_ref, kit_ref, o_ref, score_ref, bias_ref, *, s, tq, topk, q0):
    kit = kit_ref[0, :, :s].astype(BF16)
    kk = jnp.concatenate([kit, kit], axis=0)
    iw = iw_ref[0][:, IDX_DIM:IDX_DIM + IDX_HEADS] * (IDX_HEADS ** -0.5)
    score = None
    for h in range(IDX_HEADS):
        pair = qi_ref[0, :, (h // 2) * LANES:(h // 2 + 1) * LANES] * (IDX_DIM ** -0.5)
        qm = jnp.where(_head_mask(pair.shape, h), pair, 0.0).astype(BF16)
        d = jnp.dot(qm, kk, preferred_element_type=F32)
        term = iw[:, h:h + 1] * jnp.maximum(d, 0.0)
        score = term if score is None else score + term

    def vis_fn():
        qpos = q0 + lax.broadcasted_iota(jnp.int32, (tq, s), 0)
        kpos = lax.broadcasted_iota(jnp.int32, (tq, s), 1)
        return kpos <= qpos, kpos

    score_ref[:, :s] = jnp.where(vis_fn()[0], score, -jnp.inf)
    _topk_bias(score_ref, bias_ref, s, vis_fn, topk, pos_bits=int(s).bit_length())

    bias = bias_ref[:, :s]
    for hp in range(ATT_HEADS // 2):
        cols = slice(hp * LANES, (hp + 1) * LANES)
        qp = q_ref[0, :, cols] * (ATT_HEAD_DIM ** -0.5 * LOG2_E)
        kp = kt_ref[0, cols, :s].astype(BF16)
        vp = vt_ref[0, cols, :s].astype(BF16)
        outs = []
        for e in range(2):
            qm = jnp.where(_head_mask(qp.shape, e), qp, 0.0).astype(BF16)
            sc = jnp.dot(qm, kp, preferred_element_type=F32) + bias
            mx = jnp.max(sc, axis=1, keepdims=True)
            p = jnp.exp2(sc - mx)
            l = jnp.sum(p, axis=1, keepdims=True)
            outs.append(lax.dot_general(p.astype(BF16), vp, NT_DIMS, preferred_element_type=F32) / l)
        o_ref[0, :, cols] = jnp.where(_head_mask(qp.shape, 0), outs[0], outs[1]).astype(o_ref.dtype)


def _dsa_kernel(q_ref, qi_ref, iw_ref, kt_ref, vt_ref, kit_ref, o_ref, score_ref, bias_ref,
                *, tq, topk, s_quant, t):
    i = pl.program_id(1)
    q0 = i * tq
    n_var = t // s_quant
    var = (q0 + tq - 1) // s_quant
    for vi in range(n_var):
        @pl.when(var == vi)
        def _(vi=vi):
            _dsa_block(q_ref, qi_ref, iw_ref, kt_ref, vt_ref, kit_ref, o_ref, score_ref, bias_ref,
                       s=(vi + 1) * s_quant, tq=tq, topk=topk, q0=q0)


def _dsa_prompt(h3, kt, vt, kit, tq=128, s_quant=256):
    b, t, _ = h3.shape
    tq = min(tq, t)
    s_quant = min(s_quant, t)
    topk = min(TOPK_MAX, t // 4)
    qblk = lambda c: pl.BlockSpec((1, tq, ATT_WIDTH), lambda bi, i, c=c: (bi, i, c))
    feat = lambda n: pl.BlockSpec((1, n, t), lambda bi, i: (bi, 0, 0))
    kern = functools.partial(_dsa_kernel, tq=tq, topk=topk, s_quant=s_quant, t=t)
    return pl.pallas_call(
        kern,
        grid=(b, t // tq),
        in_specs=[qblk(COL_AQ // 4), qblk(PCOL_IQ // 4),
                  pl.BlockSpec((1, tq, LANES), lambda bi, i: (bi, i, PCOL_IK)),
                  feat(ATT_WIDTH), feat(ATT_WIDTH), feat(IDX_DIM)],
        out_specs=pl.BlockSpec((1, tq, ATT_WIDTH), lambda bi, i: (bi, i, 0)),
        out_shape=jax.ShapeDtypeStruct((b, t, ATT_WIDTH), BF16),
        scratch_shapes=[pltpu.VMEM((tq, t), F32), pltpu.VMEM((tq, t), F32)],
        compiler_params=_params(("parallel", "arbitrary")),
        name="dsa_prompt",
    )(h3, h3, h3, kt, vt, kit)


def _outproj_kernel(x_ref, a_ref, b_ref, wa_ref, wb_ref, g_ref, beta_ref, o_ref, *, alpha):
    mix = jnp.dot(a_ref[...].astype(BF16), wa_ref[...], preferred_element_type=F32)
    mix = mix + jnp.dot(b_ref[...].astype(BF16), wb_ref[...], preferred_element_type=F32)
    o_ref[...] = _layer_norm(alpha * x_ref[...] + mix, g_ref[...], beta_ref[...])


def _outproj_ln(x2d, o_hg, o_att, w_out, g, beta, alpha, layer):
    m, d = x2d.shape
    tm = min(512, m)
    row = lambda w: pl.BlockSpec((tm, w), lambda i: (i, 0))
    vec = pl.BlockSpec((1, d), lambda i: (0, 0))
    return pl.pallas_call(
        functools.partial(_outproj_kernel, alpha=alpha),
        grid=(m // tm,),
        in_specs=[row(d), row(HG_WIDTH), row(ATT_WIDTH),
                  pl.BlockSpec((None, HG_WIDTH, d), lambda i: (layer, 0, 0)),
                  pl.BlockSpec((None, ATT_WIDTH, d), lambda i: (layer, 1, 0)),
                  vec, vec],
        out_specs=row(d),
        out_shape=jax.ShapeDtypeStruct((m, d), F32),
        compiler_params=_params(("parallel",)),
        name="outproj_ln",
    )(x2d, o_hg, o_att, w_out, w_out, g, beta)


def _ffn_kernel(x_ref, wu_ref, cw_ref, cb_ref, wd_ref, g_ref, beta_ref, o_ref, cs_ref,
                work_ref, carry_ref, hmid_ref, *, alpha, tm, d_ff, cw):
    ti = pl.program_id(1)
    halo = SUBLANES

    @pl.when(ti == 0)
    def _():
        carry_ref[...] = jnp.zeros_like(carry_ref)

    x = x_ref[0]
    xb = x.astype(BF16)
    for j in range(d_ff // cw):
        act = []
        for e, base in enumerate((j * cw, d_ff + j * cw)):
            cols = slice(base, base + cw)
            slot = 2 * (j % 2) + e
            work_ref[slot, 0:halo, :] = carry_ref[:, cols]
            work_ref[slot, halo:halo + tm, :] = jnp.dot(xb, wu_ref[:, cols], preferred_element_type=F32)
            carry_ref[:, cols] = work_ref[slot, tm:tm + halo, :]
            c = cb_ref[:, cols]
            for tap in range(CONV_W):
                c = c + work_ref[slot, pl.ds(halo - (CONV_W - 1) + tap, tm), :] * cw_ref[tap:tap + 1, cols]
            act.append(c)
        a, gt = act
        hmid_ref[:, j * cw:(j + 1) * cw] = (a * (gt * jax.nn.sigmoid(gt))).astype(BF16)
    ffn = jnp.dot(hmid_ref[...], wd_ref[...], preferred_element_type=F32)
    o_ref[0] = _layer_norm(alpha * x + ffn, g_ref[...], beta_ref[...])

    @pl.when(ti == pl.num_programs(1) - 1)
    def _():
        cs_ref[0] = carry_ref[halo - (CONV_W - 1):, :]


def _ffn_prompt(x3, w_up, conv_w, conv_b, w_down, g, beta, alpha, layer, tm=512, cw=256):
    b, t, d = x3.shape
    d_ff = w_down.shape[1]
    tm = min(tm, t)
    kern = functools.partial(_ffn_kernel, alpha=alpha, tm=tm, d_ff=d_ff, cw=cw)
    const = lambda shape: pl.BlockSpec(shape, lambda bi, ti: (0,) * len(shape))
    return pl.pallas_call(
        kern,
        grid=(b, t // tm),
        in_specs=[pl.BlockSpec((1, tm, d), lambda bi, ti: (bi, ti, 0)),
                  _layer_spec(w_up, layer), const(conv_w.shape), const(conv_b.shape), _layer_spec(w_down, layer),
                  const(g.shape), const(beta.shape)],
        out_specs=[pl.BlockSpec((1, tm, d), lambda bi, ti: (bi, ti, 0)),
                   pl.BlockSpec((1, CONV_W - 1, 2 * d_ff), lambda bi, ti: (bi, 0, 0))],
        out_shape=[jax.ShapeDtypeStruct((b, t, d), F32),
                   jax.ShapeDtypeStruct((b, CONV_W - 1, 2 * d_ff), F32)],
        scratch_shapes=[pltpu.VMEM((4, tm + SUBLANES, cw), F32),
                        pltpu.VMEM((SUBLANES, 2 * d_ff), F32),
                        pltpu.VMEM((tm, d_ff), BF16)],
        compiler_params=_params(("parallel", "arbitrary")),
        name="ffn_prompt",
    )(x3, w_up, conv_w, conv_b, w_down, g, beta)


def _hgrn_step_kernel(q_ref, f_ref, i_ref, g_ref, lb_ref, ng_ref, s_ref, o_ref, so_ref, obuf_ref, *, nb):
    lb = lb_ref[...]
    one_m = 1.0 - lb
    hf = f_ref[...]
    f = lb + one_m * jax.nn.sigmoid(hf)
    k = one_m * jax.nn.sigmoid(-hf)
    v = i_ref[...]
    pad = jnp.zeros((HG_HEAD_DIM - nb, HG_HEAD_DIM), F32)
    col = lambda a: jnp.concatenate([a, pad], axis=0).T
    f_t, k_t, q_t = col(f), col(k), col(q_ref[...])
    for b in range(nb):
        s_new = f_t[:, b:b + 1] * s_ref[b, 0] + k_t[:, b:b + 1] * v[b:b + 1, :]
        so_ref[b, 0] = s_new
        obuf_ref[b:b + 1, :] = jnp.sum(q_t[:, b:b + 1] * s_new, axis=0, keepdims=True)
    o = obuf_ref[...]
    o = o * lax.rsqrt(jnp.mean(o * o, axis=-1, keepdims=True) + RMS_EPS) * ng_ref[...]
    hg = g_ref[...]
    o_ref[...] = (o * (hg * jax.nn.sigmoid(hg))).astype(o_ref.dtype)


def _hgrn_step(h2, lb, ng, state, layer):
    nb = h2.shape[0]
    col = lambda base: pl.BlockSpec((nb, HG_HEAD_DIM), lambda j, base=base: (0, base + j))
    sblk = pl.BlockSpec((nb, 1, HG_HEAD_DIM, HG_HEAD_DIM), lambda j: (0, j, 0, 0))
    return pl.pallas_call(
        functools.partial(_hgrn_step_kernel, nb=nb),
        grid=(HG_HEADS,),
        in_specs=[col(COL_HQ), col(COL_HF), col(COL_HI), col(COL_HG),
                  pl.BlockSpec((1, HG_HEAD_DIM), lambda j: (0, j)),
                  pl.BlockSpec((1, HG_HEAD_DIM), lambda j: (0, 0)),
                  pl.BlockSpec((None, nb, 1, HG_HEAD_DIM, HG_HEAD_DIM), lambda j: (layer, 0, j, 0, 0))],
        out_specs=[pl.BlockSpec((nb, HG_HEAD_DIM), lambda j: (0, j)), sblk],
        out_shape=[jax.ShapeDtypeStruct((nb, HG_WIDTH), BF16),
                   jax.ShapeDtypeStruct(state.shape[1:], F32)],
        scratch_shapes=[pltpu.VMEM((nb, HG_HEAD_DIM), F32)],
        compiler_params=_params(("parallel",)),
        name="hgrn_step",
    )(h2, h2, h2, h2, lb, ng, state)


def _sscore_kernel(pt_ref, qi_ref, kin_ref, iw_ref, kidx_hbm, sc_ref, sself_ref, kbuf, sem, *, n_pages, layer):
    b = pl.program_id(0)
    nb = pl.num_programs(0)

    def copies(bb, slot):
        return [pltpu.make_async_copy(kidx_hbm.at[layer, pt_ref[bb * n_pages + p]], kbuf.at[slot, p],
                                      sem.at[slot]) for p in range(n_pages)]

    @pl.when(b == 0)
    def _():
        for cp in copies(0, 0):
            cp.start()

    @pl.when(b + 1 < nb)
    def _():
        for cp in copies(b + 1, (b + 1) % 2):
            cp.start()

    slot = b % 2
    for cp in copies(b, slot):
        cp.wait()

    qi = qi_ref[0] * (IDX_DIM ** -0.5)
    kin = kin_ref[0]
    iw = iw_ref[0] * (IDX_HEADS ** -0.5)
    sc = jnp.zeros((n_pages, PAGE_SIZE), F32)
    s_self = jnp.zeros((1, 1), F32)
    for h in range(IDX_HEADS):
        row = qi[h:h + 1, :]
        d = jnp.sum(kbuf[slot] * _row_to_col(row)[None], axis=1)
        w_h = iw[:, h:h + 1]
        sc = sc + w_h * jnp.maximum(d, 0.0)
        d_self = jnp.sum(row * kin, axis=1, keepdims=True)
        s_self = s_self + w_h * jnp.maximum(d_self, 0.0)
    sc_ref[0] = sc
    sself_ref[0] = jnp.broadcast_to(s_self, (1, LANES))


def _sample_scores(qi3, kin3, iw3, kidx, page_table_flat, n_pages, layer):
    nb = qi3.shape[0]
    kern = functools.partial(_sscore_kernel, n_pages=n_pages, layer=layer)
    per_b = lambda a: pl.BlockSpec((1,) + a.shape[1:], lambda b, pt: (b, 0, 0))
    grid_spec = pltpu.PrefetchScalarGridSpec(
        num_scalar_prefetch=1,
        grid=(nb,),
        in_specs=[per_b(qi3), per_b(kin3), per_b(iw3), pl.BlockSpec(memory_space=pl.ANY)],
        out_specs=[pl.BlockSpec((1, n_pages, PAGE_SIZE), lambda b, pt: (b, 0, 0)),
                   pl.BlockSpec((1, 1, LANES), lambda b, pt: (b, 0, 0))],
        scratch_shapes=[pltpu.VMEM((2, n_pages, IDX_DIM, PAGE_SIZE), F32),
                        pltpu.SemaphoreType.DMA((2,))],
    )
    return pl.pallas_call(
        kern,
        grid_spec=grid_spec,
        out_shape=[jax.ShapeDtypeStruct((nb, n_pages, PAGE_SIZE), F32),
                   jax.ShapeDtypeStruct((nb, 1, LANES), F32)],
        compiler_params=_params(("arbitrary",)),
        name="sample_scores",
    )(page_table_flat, qi3, kin3, iw3, kidx)


def _sselect_kernel(sc_ref, bias_ref, *, n_vis, topk):
    rows, s = sc_ref.shape

    def vis_fn():
        kpos = lax.broadcasted_iota(jnp.int32, (rows, s), 1)
        return kpos < n_vis, kpos

    _topk_bias(sc_ref, bias_ref, s, vis_fn, topk, pos_bits=int(s).bit_length())


def _sample_select(score2d, n_vis, topk):
    full = pl.BlockSpec(score2d.shape, lambda i: (0, 0))
    return pl.pallas_call(
        functools.partial(_sselect_kernel, n_vis=n_vis, topk=topk),
        grid=(1,),
        in_specs=[full],
        out_specs=full,
        out_shape=jax.ShapeDtypeStruct(score2d.shape, F32),
        compiler_params=_params(("arbitrary",)),
        name="sample_select",
    )(score2d)


def _satt_kernel(pt_ref, q_ref, kn_ref, vn_ref, bias_ref, bself_ref, k_hbm, v_hbm, o_ref,
                 buf, prob_ref, acc_ref, l_ref, oself_ref, sem, *, n_pages, cp_pages, layer):
    b = pl.program_id(0)
    s = pl.program_id(1)
    ns = pl.num_programs(1)
    nc = n_pages // cp_pages
    step = b * ns + s
    n_steps = pl.num_programs(0) * ns
    scale = ATT_HEAD_DIM ** -0.5

    def copies(src, bb, chunk, slot):
        return [pltpu.make_async_copy(src.at[layer, pt_ref[bb * n_pages + chunk * cp_pages + p]], buf.at[slot, p],
                                      sem.at[slot]) for p in range(cp_pages)]

    def for_stage(st, slot, fn):
        bb = st // ns
        ss = st % ns

        @pl.when(ss < nc)
        def _():
            for cp in copies(k_hbm, bb, ss, slot):
                fn(cp)

        @pl.when(ss >= nc)
        def _():
            for cp in copies(v_hbm, bb, ss - nc, slot):
                fn(cp)

    @pl.when(step == 0)
    def _():
        for_stage(0, 0, lambda cp: cp.start())

    @pl.when(step + 1 < n_steps)
    def _():
        for_stage(step + 1, (step + 1) % 2, lambda cp: cp.start())

    slot = step % 2
    for_stage(step, slot, lambda cp: cp.wait())

    q = q_ref[0] * scale

    @pl.when(s < nc)
    def _():
        r0 = pl.multiple_of(s * cp_pages, cp_pages)
        bias = bias_ref[0, pl.ds(r0, cp_pages), :]
        for h in range(ATT_HEADS):
            qc = _row_to_col(q[h:h + 1, :])
            prob_ref[h, pl.ds(r0, cp_pages), :] = jnp.sum(buf[slot, :, h] * qc[None], axis=1) + bias

    @pl.when(s == nc)
    def _():
        lg = prob_ref[...]
        s_self = jnp.sum(q * kn_ref[0], axis=1, keepdims=True) + bself_ref[0][:, :1]
        mx = jnp.maximum(jnp.max(jnp.max(lg, axis=1), axis=1, keepdims=True), s_self)
        p = jnp.exp(lg - mx[:, :, None])
        p_self = jnp.exp(s_self - mx)
        prob_ref[...] = p
        l_ref[...] = jnp.sum(jnp.sum(p, axis=1), axis=1, keepdims=True) + p_self
        oself_ref[...] = p_self * vn_ref[0]
        acc_ref[...] = jnp.zeros_like(acc_ref)

    @pl.when(s >= nc)
    def _():
        r0 = pl.multiple_of((s - nc) * cp_pages, cp_pages)
        for h in range(ATT_HEADS):
            p = prob_ref[h, pl.ds(r0, cp_pages), :]
            acc_ref[h] += jnp.sum(buf[slot, :, h] * p[:, None, :], axis=0)

    @pl.when(s == ns - 1)
    def _():
        for h in range(ATT_HEADS):
            row = _col_to_row(jnp.sum(acc_ref[h], axis=1, keepdims=True))
            o_ref[0, h:h + 1, :] = (row + oself_ref[h:h + 1, :]) / l_ref[h:h + 1, :]


def _sample_attend(q3, kn3, vn3, bias, bself, k_l, v_l, page_table_flat, n_pages, layer, cp_pages=32):
    nb = q3.shape[0]
    cp_pages = min(cp_pages, n_pages)
    nc = n_pages // cp_pages
    kern = functools.partial(_satt_kernel, n_pages=n_pages, cp_pages=cp_pages, layer=layer)
    per_b = lambda a: pl.BlockSpec((1,) + a.shape[1:], lambda b, s, pt: (b, 0, 0))
    grid_spec = pltpu.PrefetchScalarGridSpec(
        num_scalar_prefetch=1,
        grid=(nb, 2 * nc),
        in_specs=[per_b(q3), per_b(kn3), per_b(vn3), per_b(bias), per_b(bself),
                  pl.BlockSpec(memory_space=pl.ANY), pl.BlockSpec(memory_space=pl.ANY)],
        out_specs=per_b(q3),
        scratch_shapes=[pltpu.VMEM((2, cp_pages, ATT_HEADS, ATT_HEAD_DIM, PAGE_SIZE), F32),
                        pltpu.VMEM((ATT_HEADS, n_pages, PAGE_SIZE), F32),
                        pltpu.VMEM((ATT_HEADS, ATT_HEAD_DIM, PAGE_SIZE), F32),
                        pltpu.VMEM((ATT_HEADS, 1), F32),
                        pltpu.VMEM((ATT_HEADS, ATT_HEAD_DIM), F32),
                        pltpu.SemaphoreType.DMA((2,))],
    )
    return pl.pallas_call(
        kern,
        grid_spec=grid_spec,
        out_shape=jax.ShapeDtypeStruct(q3.shape, F32),
        compiler_params=_params(("arbitrary", "arbitrary")),
        name="sample_attend",
    )(page_table_flat, q3, kn3, vn3, bias, bself, k_l, v_l)


def _ffn_step_kernel(x_ref, wu_ref, cw_ref, cb_ref, wd_ref, g_ref, beta_ref, buf_ref, o_ref, nbuf_ref,
                     *, alpha, d_ff):
    x = x_ref[...]
    u = jnp.dot(x.astype(BF16), wu_ref[...], preferred_element_type=F32)
    prev = [buf_ref[:, j, :] for j in range(CONV_W - 1)]
    ext = prev + [u]
    c = cb_ref[...]
    for tap in range(CONV_W):
        c = c + ext[tap] * cw_ref[tap:tap + 1, :]
    a, gt = c[:, :d_ff], c[:, d_ff:]
    hmid = (a * (gt * jax.nn.sigmoid(gt))).astype(BF16)
    ffn = jnp.dot(hmid, wd_ref[...], preferred_element_type=F32)
    o_ref[...] = _layer_norm(alpha * x + ffn, g_ref[...], beta_ref[...])
    for j in range(CONV_W - 1):
        nbuf_ref[:, j, :] = ext[j + 1]


def _ffn_step(x2d, w_up, conv_w, conv_b, w_down, g, beta, buf, alpha, layer):
    nb, d = x2d.shape
    d_ff = w_down.shape[1]
    full = lambda a: pl.BlockSpec(a.shape, lambda i: (0,) * a.ndim)
    return pl.pallas_call(
        functools.partial(_ffn_step_kernel, alpha=alpha, d_ff=d_ff),
        grid=(1,),
        in_specs=[full(x2d), _layer_spec(w_up, layer), full(conv_w), full(conv_b), _layer_spec(w_down, layer),
                  full(g), full(beta), _layer_spec(buf, layer)],
        out_specs=[pl.BlockSpec((nb, d), lambda i: (0, 0)),
                   pl.BlockSpec(buf.shape[1:], lambda i: (0, 0, 0))],
        out_shape=[jax.ShapeDtypeStruct((nb, d), F32), jax.ShapeDtypeStruct(buf.shape[1:], F32)],
        compiler_params=_params(("arbitrary",)),
        name="ffn_step",
    )(x2d, w_up, conv_w, conv_b, w_down, g, beta, buf)


def _lower_bounds(lb_param):
    p = jax.nn.softmax(lb_param.astype(F32), axis=0)
    return jnp.cumsum(p, axis=0) - p[0:1]


def kernel(x_prompt, x_sample, cache_k, cache_v, cache_kidx, state_hgrn, state_conv, page_table, w_in, lb_param,
           hg_norm_g, w_out, ln1_g, ln1_b, w_up, conv_w, conv_b, w_down, ln2_g, ln2_b):
    depth = w_in.shape[0]
    b, t, d = x_prompt.shape
    nb = x_sample.shape[0]
    assert x_sample.shape[1] == 1
    n_pages = page_table.shape[1]
    past = n_pages * PAGE_SIZE
    alpha = (2 * depth) ** 0.25

    lbs = _lower_bounds(lb_param)
    pad = jnp.zeros(w_in.shape[:-1] + (N_FULL - w_in.shape[-1],), w_in.dtype)
    w_full = jnp.concatenate([w_in, pad], axis=-1).astype(BF16)
    w_in_t = jnp.swapaxes(w_in, 1, 2)
    w_feat = jnp.concatenate([w_in_t[:, COL_AK * LANES:COL_IQ * LANES],
                              w_in_t[:, COL_IK * LANES:COL_IK * LANES + IDX_DIM]], axis=1).astype(BF16)
    w_out_b, w_up_b, w_down_b = w_out.astype(BF16), w_up.astype(BF16), w_down.astype(BF16)
    pt_flat = page_table.reshape(-1).astype(jnp.int32)
    kidx_v = jnp.swapaxes(cache_kidx, 2, 3)
    k_v = jnp.transpose(cache_k, (0, 1, 3, 4, 2))
    v_v = jnp.transpose(cache_v, (0, 1, 3, 4, 2))
    topk_s = min(TOPK_MAX, (past + 1) // 4)
    first_lane = jnp.arange(LANES) == 0

    xp = x_prompt
    xs = x_sample.reshape(nb, d)
    outs = {n: [] for n in ("kp", "vp", "kip", "hgp", "cbp", "ks", "vs", "kis", "hgs", "cbs")}
    seg = lambda h, c0, w: h[..., c0 * LANES:c0 * LANES + w]

    for l in range(depth):
        vec = lambda a: a[l][None, :]
        lb_l, ng_l = vec(lbs), vec(hg_norm_g)

        h3, kt, vt, kit = _inproj_prompt(xp, w_full, w_feat, l)
        o_hg, s_new = _hgrn_prompt(h3, lb_l, ng_l)
        o_att = _dsa_prompt(h3, kt, vt, kit)
        x1 = _outproj_ln(xp.reshape(b * t, d), o_hg.reshape(b * t, HG_WIDTH), o_att.reshape(b * t, ATT_WIDTH),
                         w_out_b, vec(ln1_g), vec(ln1_b), alpha, l)
        xp, cb_new = _ffn_prompt(x1.reshape(b, t, d), w_up_b, conv_w[l], vec(conv_b), w_down_b,
                                 vec(ln2_g), vec(ln2_b), alpha, l)
        outs["kp"].append(kt)
        outs["vp"].append(vt)
        outs["kip"].append(kit)
        outs["hgp"].append(s_new)
        outs["cbp"].append(cb_new)

        hs = _inproj_sample(xs, w_full, l)
        o_hg_s, s_new_s = _hgrn_step(hs, lb_l, ng_l, state_hgrn, l)
        heads = lambda c0: seg(hs, c0, ATT_WIDTH).reshape(nb, ATT_HEADS, ATT_HEAD_DIM)
        kin3 = seg(hs, COL_IK, IDX_DIM).reshape(nb, 1, IDX_DIM)
        iw3 = hs[:, COL_IK * LANES + IDX_DIM:COL_IK * LANES + IDX_DIM + IDX_HEADS].reshape(nb, 1, IDX_HEADS)
        sc, sself = _sample_scores(heads(COL_IQ), kin3, iw3, kidx_v, pt_flat, n_pages, l)
        score2d = jnp.concatenate([sc.reshape(nb, past),
                                   jnp.where(first_lane, sself.reshape(nb, LANES), -jnp.inf)], axis=1)
        bias2d = _sample_select(score2d, past + 1, topk_s)
        bias = bias2d[:, :past].reshape(nb, n_pages, PAGE_SIZE)
        bself = bias2d[:, past:].reshape(nb, 1, LANES)
        kn3, vn3 = heads(COL_AK), heads(COL_AV)
        o_att_s = _sample_attend(heads(COL_AQ), kn3, vn3, bias, bself, k_v, v_v, pt_flat, n_pages, l)
        x1s = _outproj_ln(xs, o_hg_s, o_att_s.reshape(nb, ATT_WIDTH), w_out_b, vec(ln1_g), vec(ln1_b), alpha, l)
        xs, cb_new_s = _ffn_step(x1s, w_up_b, conv_w[l], vec(conv_b), w_down_b, vec(ln2_g), vec(ln2_b),
                                 state_conv, alpha, l)
        outs["ks"].append(kn3.reshape(nb, 1, ATT_HEADS, ATT_HEAD_DIM))
        outs["vs"].append(vn3.reshape(nb, 1, ATT_HEADS, ATT_HEAD_DIM))
        outs["kis"].append(kin3)
        outs["hgs"].append(s_new_s)
        outs["cbs"].append(cb_new_s)

    st = {n: jnp.stack(v) for n, v in outs.items()}
    k_p = jnp.transpose(st["kp"].reshape(depth, b, ATT_HEADS, ATT_HEAD_DIM, t), (0, 1, 4, 2, 3))
    v_p = jnp.transpose(st["vp"].reshape(depth, b, ATT_HEADS, ATT_HEAD_DIM, t), (0, 1, 4, 2, 3))
    ki_p = jnp.swapaxes(st["kip"], 2, 3)
    return (xp, xs.reshape(nb, 1, d), k_p, v_p, ki_p, st["hgp"], st["cbp"],
            st["ks"], st["vs"], st["kis"], st["hgs"], st["cbs"])
```
